```python
import jax, jax.numpy as jnp
from jax import lax
import numpy as np

D_MODEL = 2048
BATCH = 4
SEQ = 2048
DEPTH = 4
DEC_BATCH = 128
DEC_SEQ = 1
PAST_LEN = 8192
PAGE_SIZE = 128

MLA_HEADS = 8
MLA_NOPE = 128
MLA_ROPE = 64
MLA_V = 128
MLA_QK = MLA_NOPE + MLA_ROPE
Q_LORA = 512
KV_LORA = 512
CONV_CH = 1024
CONV_WIDTH = 31
EVEN_IN = Q_LORA + KV_LORA + MLA_ROPE + 2 * CONV_CH
MIX_WIDTH = MLA_HEADS * MLA_V + CONV_CH
MOBA_HEADS = 16
MOBA_KV_HEADS = 2
MOBA_GROUP = MOBA_HEADS // MOBA_KV_HEADS
HEAD_DIM = 128
MOBA_BLOCK = 256
MOBA_TOPK = 3
MOBA_QBLK = 16
ODD_IN = (MOBA_HEADS + 2 * MOBA_KV_HEADS) * HEAD_DIM
N_GROUPS = 4
EXPERTS_PER_GROUP = 8
EXPERT_TOPK = 2
EXPERT_FF = 512
ATTN_QBLK = 128
ROPE_THETA = 10000.0
EPS = 1e-6
N_EVEN = (DEPTH + 1) // 2
N_ODD = DEPTH // 2

kernel_name = 'hybrid_mla_conformer_moba_hmoe_step'

F32 = jnp.float32


def rmsnorm(x, g):
    xf = x.astype(F32)
    y = xf * lax.rsqrt(jnp.mean(xf * xf, axis=-1, keepdims=True) + EPS)
    return (y * g.astype(F32)).astype(x.dtype)


def layernorm(x, g, b):
    xf = x.astype(F32)
    mu = jnp.mean(xf, axis=-1, keepdims=True)
    var = jnp.mean(jnp.square(xf - mu), axis=-1, keepdims=True)
    y = (xf - mu) * lax.rsqrt(var + EPS)
    return (y * g.astype(F32) + b.astype(F32)).astype(x.dtype)


def ada_modulation(c, w, b):
    return jnp.split(jax.nn.silu(c) @ w + b, 6, axis=-1)


def modulate(xn, shift, scale):
    return xn * (1 + scale[:, None, :]) + shift[:, None, :]


def rope(x, pos):
    half = x.shape[-1] // 2
    inv = ROPE_THETA ** (-jnp.arange(half, dtype=F32) / half)
    ang = pos.astype(F32)[:, None] * inv[None, :]
    cos = jnp.cos(ang)[:, None, :]
    sin = jnp.sin(ang)[:, None, :]
    xf = x.astype(F32)
    x1, x2 = xf[..., :half], xf[..., half:]
    return jnp.concatenate([x1 * cos - x2 * sin, x2 * cos + x1 * sin], axis=-1).astype(x.dtype)


def causal_dwconv(u_padded, w, b):
    y = lax.conv_general_dilated(u_padded, w[:, None, :], (1,), 'VALID',
                                 dimension_numbers=('NWC', 'WIO', 'NWC'),
                                 feature_group_count=u_padded.shape[-1])
    return y + b


def conv_module_tail(y, ln_g, ln_b):
    z = layernorm(y, ln_g, ln_b)
    return z * jax.nn.sigmoid(z)


def blocked_causal_attention(q, k, v, scale):
    n, s_len, nh, _ = q.shape
    kpos = jnp.arange(s_len)

    def one_block(i):
        qs = i * ATTN_QBLK
        qb = lax.dynamic_slice_in_dim(q, qs, ATTN_QBLK, axis=1)
        s = jnp.einsum('nqhd,nkhd->nhqk', qb, k).astype(F32) * scale
        qpos = qs + jnp.arange(ATTN_QBLK)
        s = jnp.where(kpos[None, :] <= qpos[:, None], s, -jnp.inf)
        p = jax.nn.softmax(s, axis=-1).astype(v.dtype)
        return jnp.einsum('nhqk,nkhd->nqhd', p, v)

    out = lax.map(one_block, jnp.arange(s_len // ATTN_QBLK))
    return out.transpose(1, 0, 2, 3, 4).reshape(n, s_len, nh, v.shape[-1])


def mla_decode(q_cat, cache, layer, page_table, new_rows, scale):
    n, nq, nh, _ = q_cat.shape
    qf = q_cat.astype(F32)

    def merge(carry, rows, mask):
        m, l, acc = carry
        s = jnp.einsum('nqhc,nkc->nhqk', qf, rows) * scale
        if mask is not None:
            s = jnp.where(mask, s, -jnp.inf)
        m_new = jnp.maximum(m, jnp.max(s, axis=-1))
        corr = jnp.exp(m - m_new)
        p = jnp.exp(s - m_new[..., None])
        return (m_new, l * corr + jnp.sum(p, axis=-1),
                acc * corr[..., None] + jnp.einsum('nhqk,nkc->nhqc', p, rows[..., :KV_LORA]))

    def step(carry, pages):
        return merge(carry, cache[layer, pages].astype(F32), None), None

    init = (jnp.full((n, nh, nq), -jnp.inf, F32), jnp.zeros((n, nh, nq), F32),
            jnp.zeros((n, nh, nq, KV_LORA), F32))
    carry, _ = lax.scan(step, init, page_table.T)
    causal = jnp.arange(nq)[:, None] >= jnp.arange(nq)[None, :]
    _, l, acc = merge(carry, new_rows.astype(F32), causal)
    return (acc / l[..., None]).transpose(0, 2, 1, 3).astype(q_cat.dtype)


def even_project(h, pos, w_in, g_q, w_qb, g_kv):
    n, s_len, _ = h.shape
    proj = h @ w_in
    cq = proj[..., :Q_LORA]
    ckv = proj[..., Q_LORA:Q_LORA + KV_LORA]
    kpe = proj[..., Q_LORA + KV_LORA:Q_LORA + KV_LORA + MLA_ROPE]
    glu_a, glu_b = jnp.split(proj[..., Q_LORA + KV_LORA + MLA_ROPE:], 2, axis=-1)
    q = (rmsnorm(cq, g_q) @ w_qb).reshape(n, s_len, MLA_HEADS, MLA_QK)
    q_nope = q[..., :MLA_NOPE]
    q_pe = rope(q[..., MLA_NOPE:], pos)
    c_kv = rmsnorm(ckv, g_kv)
    k_pe = rope(kpe[:, :, None, :], pos)[:, :, 0, :]
    u = glu_a * jax.nn.sigmoid(glu_b)
    return q_nope, q_pe, c_kv, k_pe, u


def even_mixer_prompt(h, pos, w_in, g_q, w_qb, g_kv, w_kvb, conv_w, conv_b, ln_g, ln_b, w_out):
    n, s_len, _ = h.shape
    q_nope, q_pe, c_kv, k_pe, u = even_project(h, pos, w_in, g_q, w_qb, g_kv)
    kv = (c_kv @ w_kvb).reshape(n, s_len, MLA_HEADS, MLA_NOPE + MLA_V)
    q = jnp.concatenate([q_nope, q_pe], axis=-1)
    k = jnp.concatenate([kv[..., :MLA_NOPE],
                         jnp.broadcast_to(k_pe[:, :, None, :], (n, s_len, MLA_HEADS, MLA_ROPE))], axis=-1)
    attn = blocked_causal_attention(q, k, kv[..., MLA_NOPE:], MLA_QK ** -0.5)
    u_pad = jnp.concatenate([jnp.zeros((n, CONV_WIDTH - 1, CONV_CH), u.dtype), u], axis=1)
    conv = conv_module_tail(causal_dwconv(u_pad, conv_w, conv_b), ln_g, ln_b)
    out = jnp.concatenate([attn.reshape(n, s_len, MLA_HEADS * MLA_V), conv], axis=-1) @ w_out
    return out, jnp.concatenate([c_kv, k_pe], axis=-1), u_pad[:, -(CONV_WIDTH - 1):]


def even_mixer_sample(h, pos, cache_mla, conv_state, layer, page_table,
                      w_in, g_q, w_qb, g_kv, w_kvb, conv_w, conv_b, ln_g, ln_b, w_out):
    n, nq, _ = h.shape
    q_nope, q_pe, c_kv, k_pe, u = even_project(h, pos, w_in, g_q, w_qb, g_kv)
    w_kvb_h = w_kvb.reshape(KV_LORA, MLA_HEADS, MLA_NOPE + MLA_V)
    q_lat = jnp.einsum('nqhd,chd->nqhc', q_nope, w_kvb_h[..., :MLA_NOPE])
    q_cat = jnp.concatenate([q_lat, q_pe], axis=-1)
    new_rows = jnp.concatenate([c_kv, k_pe], axis=-1)
    o_lat = mla_decode(q_cat, cache_mla, layer, page_table, new_rows, MLA_QK ** -0.5)
    attn = jnp.einsum('nqhc,chv->nqhv', o_lat, w_kvb_h[..., MLA_NOPE:])
    u_pad = jnp.concatenate([conv_state.astype(u.dtype), u], axis=1)
    conv = conv_module_tail(causal_dwconv(u_pad, conv_w, conv_b), ln_g, ln_b)
    out = jnp.concatenate([attn.reshape(n, nq, MLA_HEADS * MLA_V), conv], axis=-1) @ w_out
    return out, new_rows, u_pad[:, -(CONV_WIDTH - 1):]


def odd_project(h, pos, w_qkv):
    n, s_len, _ = h.shape
    qkv = h @ w_qkv
    nqw = MOBA_HEADS * HEAD_DIM
    nkw = MOBA_KV_HEADS * HEAD_DIM
    q = rope(qkv[..., :nqw].reshape(n, s_len, MOBA_HEADS, HEAD_DIM), pos)
    k = rope(qkv[..., nqw:nqw + nkw].reshape(n, s_len, MOBA_KV_HEADS, HEAD_DIM), pos)
    v = qkv[..., nqw + nkw:].reshape(n, s_len, MOBA_KV_HEADS, HEAD_DIM)
    return q, k, v


def moba_prompt(q, k, v):
    n, s_len = q.shape[:2]
    n_blk = -(-s_len // MOBA_BLOCK)
    pad = n_blk * MOBA_BLOCK - s_len
    kv_of_head = jnp.arange(MOBA_HEADS) // MOBA_GROUP
    scale = HEAD_DIM ** -0.5

    def blocks(t):
        t = jnp.pad(t, ((0, 0), (0, pad), (0, 0), (0, 0)))
        return t.reshape(n, n_blk, MOBA_BLOCK, MOBA_KV_HEADS, HEAD_DIM).transpose(0, 3, 1, 2, 4)

    kb, vb = blocks(k), blocks(v)
    kmean = jnp.mean(kb.astype(F32), axis=3)[:, kv_of_head]
    ksel = min(MOBA_TOPK, n_blk)
    bi = jnp.arange(n)[:, None, None, None]
    hi = kv_of_head[None, :, None, None]
    blk_ids = jnp.arange(n_blk)

    def one_block(i):
        qs = i * MOBA_QBLK
        qb = lax.dynamic_slice_in_dim(q, qs, MOBA_QBLK, axis=1).transpose(0, 2, 1, 3)
        pos = qs + jnp.arange(MOBA_QBLK)
        own = pos // MOBA_BLOCK
        gate = jnp.einsum('nhqd,nhbd->nhqb', qb.astype(F32), kmean)
        gate = jnp.where(blk_ids[None, :] < own[:, None], gate, -jnp.inf)
        _, idx = lax.top_k(gate, ksel)
        valid = jnp.arange(ksel)[None, :] < own[:, None]
        kg = kb[bi, hi, idx]
        vg = vb[bi, hi, idx]
        s_sel = jnp.einsum('nhqd,nhqckd->nhqck', qb, kg).astype(F32) * scale
        s_sel = jnp.where(valid[:, :, None], s_sel, -jnp.inf).reshape(n, MOBA_HEADS, MOBA_QBLK, ksel * MOBA_BLOCK)
        ob = qs // MOBA_BLOCK
        kown = lax.dynamic_index_in_dim(kb, ob, axis=2, keepdims=False)[:, kv_of_head]
        vown = lax.dynamic_index_in_dim(vb, ob, axis=2, keepdims=False)[:, kv_of_head]
        s_own = jnp.einsum('nhqd,nhkd->nhqk', qb, kown).astype(F32) * scale
        kpos = ob * MOBA_BLOCK + jnp.arange(MOBA_BLOCK)
        s_own = jnp.where(kpos[None, :] <= pos[:, None], s_own, -jnp.inf)
        p = jax.nn.softmax(jnp.concatenate([s_sel, s_own], axis=-1), axis=-1).astype(v.dtype)
        p_sel = p[..., :ksel * MOBA_BLOCK].reshape(n, MOBA_HEADS, MOBA_QBLK, ksel, MOBA_BLOCK)
        o = (jnp.einsum('nhqck,nhqckd->nhqd', p_sel, vg)
             + jnp.einsum('nhqk,nhkd->nhqd', p[..., ksel * MOBA_BLOCK:], vown))
        return o.transpose(0, 2, 1, 3)

    out = lax.map(one_block, jnp.arange(s_len // MOBA_QBLK))
    return out.transpose(1, 0, 2, 3, 4).reshape(n, s_len, MOBA_HEADS, HEAD_DIM)


def moba_sample(q, k_new, v_new, cache_k, cache_v, layer, page_table):
    n, nq = q.shape[:2]
    n_pages = page_table.shape[1]
    past_len = n_pages * PAGE_SIZE
    ppb = MOBA_BLOCK // PAGE_SIZE
    n_fb = past_len // MOBA_BLOCK
    t0 = n_fb * MOBA_BLOCK
    n_tp = n_pages - n_fb * ppb
    kv_of_head = jnp.arange(MOBA_HEADS) // MOBA_GROUP
    scale = HEAD_DIM ** -0.5
    page_mean = jnp.mean(cache_k[layer, page_table[:, :n_fb * ppb]].astype(F32), axis=2)
    past_mean = jnp.mean(page_mean.reshape(n, n_fb, ppb, MOBA_KV_HEADS, HEAD_DIM), axis=2)
    n_tail = n_tp * PAGE_SIZE + nq
    n_tb = -(-n_tail // MOBA_BLOCK)
    lt = n_tb * MOBA_BLOCK
    tail_pt = page_table[:, n_fb * ppb:]

    def tail_rows(cache, new):
        past = cache[layer, tail_pt].reshape(n, n_tp * PAGE_SIZE, MOBA_KV_HEADS, HEAD_DIM).astype(new.dtype)
        rows = jnp.concatenate([past, new], axis=1)
        return jnp.pad(rows, ((0, 0), (0, lt - n_tail), (0, 0), (0, 0)))

    tk, tv = tail_rows(cache_k, k_new), tail_rows(cache_v, v_new)
    tail_mean = jnp.mean(tk.reshape(n, n_tb, MOBA_BLOCK, MOBA_KV_HEADS, HEAD_DIM).astype(F32), axis=2)
    means = jnp.concatenate([past_mean, tail_mean], axis=1)[:, :, kv_of_head].transpose(0, 2, 1, 3)
    nb = n_fb + n_tb
    pos = past_len + jnp.arange(nq)
    own = pos // MOBA_BLOCK
    qh = q.transpose(0, 2, 1, 3)
    gate = jnp.einsum('nhqd,nhbd->nhqb', qh.astype(F32), means)
    gate = jnp.where(jnp.arange(nb)[None, :] < own[:, None], gate, -jnp.inf)
    ksel = min(MOBA_TOPK, nb)
    _, idx = lax.top_k(gate, ksel)
    valid = jnp.arange(ksel)[None, :] < own[:, None]
    from_past = valid & (idx < n_fb)
    lpage = jnp.minimum(jnp.clip(idx, 0, max(n_fb - 1, 0))[..., None] * ppb + jnp.arange(ppb), n_pages - 1)
    phys = page_table[jnp.arange(n)[:, None, None, None, None], lpage]
    hsel = kv_of_head[None, :, None, None, None]
    kg = cache_k[layer, phys, :, hsel].reshape(n, MOBA_HEADS, nq, ksel, MOBA_BLOCK, HEAD_DIM).astype(q.dtype)
    vg = cache_v[layer, phys, :, hsel].reshape(n, MOBA_HEADS, nq, ksel, MOBA_BLOCK, HEAD_DIM).astype(q.dtype)
    s_past = jnp.einsum('nhqd,nhqckd->nhqck', qh, kg).astype(F32) * scale
    s_past = jnp.where(from_past[..., None], s_past, -jnp.inf).reshape(n, MOBA_HEADS, nq, ksel * MOBA_BLOCK)
    tk_h = tk[:, :, kv_of_head].transpose(0, 2, 1, 3)
    tv_h = tv[:, :, kv_of_head].transpose(0, 2, 1, 3)
    s_tail = jnp.einsum('nhqd,nhkd->nhqk', qh, tk_h).astype(F32) * scale
    tpos = t0 + jnp.arange(lt)
    tblk = n_fb + jnp.arange(lt) // MOBA_BLOCK
    sel_tail = jnp.any((idx[..., None] == tblk) & valid[:, :, None], axis=3)
    allow = (tpos[None, :] <= pos[:, None]) & ((tblk[None, :] == own[:, None]) | sel_tail)
    s_tail = jnp.where(allow, s_tail, -jnp.inf)
    p = jax.nn.softmax(jnp.concatenate([s_past, s_tail], axis=-1), axis=-1).astype(q.dtype)
    p_past = p[..., :ksel * MOBA_BLOCK].reshape(n, MOBA_HEADS, nq, ksel, MOBA_BLOCK)
    o = (jnp.einsum('nhqck,nhqckd->nhqd', p_past, vg)
         + jnp.einsum('nhqk,nhkd->nhqd', p[..., ksel * MOBA_BLOCK:], tv_h))
    return o.transpose(0, 2, 1, 3)


def odd_mixer_prompt(h, pos, w_qkv, w_o):
    n, s_len, _ = h.shape
    q, k, v = odd_project(h, pos, w_qkv)
    o = moba_prompt(q, k, v)
    return o.reshape(n, s_len, MOBA_HEADS * HEAD_DIM) @ w_o, k, v


def odd_mixer_sample(h, pos, cache_k, cache_v, layer, page_table, w_qkv, w_o):
    n, nq, _ = h.shape
    q, k, v = odd_project(h, pos, w_qkv)
    o = moba_sample(q, k, v, cache_k, cache_v, layer, page_table)
    return o.reshape(n, nq, MOBA_HEADS * HEAD_DIM) @ w_o, k, v


def hier_moe(h, w_group, b_group, w_sec, b_sec, w_gate, w_up, w_down):
    n, s_len, d = h.shape
    t = h.reshape(n * s_len, d)
    g_prob = jax.nn.softmax((t @ w_group).astype(F32) + b_group.astype(F32), axis=-1)
    g_val, g_idx = lax.top_k(g_prob, 1)
    e_logit = ((t @ w_sec).astype(F32) + b_sec.astype(F32)).reshape(-1, N_GROUPS, EXPERTS_PER_GROUP)
    e_logit = jnp.take_along_axis(e_logit, g_idx[:, :, None], axis=1)[:, 0]
    e_val, e_idx = lax.top_k(e_logit, EXPERT_TOPK)
    e_w = jax.nn.softmax(e_val, axis=-1) * g_val
    within = jnp.sum(jax.nn.one_hot(e_idx, EXPERTS_PER_GROUP, dtype=F32) * e_w[..., None], axis=1)
    comb = (jax.nn.one_hot(g_idx[:, 0], N_GROUPS, dtype=F32)[:, :, None] * within[:, None, :]).astype(t.dtype)
    y = jnp.zeros_like(t)
    for g in range(N_GROUPS):
        a = jax.nn.silu(jnp.einsum('td,edf->tef', t, w_gate[g])) * jnp.einsum('td,edf->tef', t, w_up[g])
        y = y + jnp.einsum('tef,efd->td', a * comb[:, g, :, None], w_down[g])
    return y.reshape(n, s_len, d)


def setup_inputs(seed: int = 0) -> dict:
    key = jax.random.key(seed)
    keys = jax.random.split(key, 64)
    counter = [0]

    def nk():
        counter[0] += 1
        return keys[counter[0] - 1]

    def nrm(shape, scale):
        return jax.random.normal(nk(), shape, F32) * scale

    def gain(shape):
        return 1.0 + nrm(shape, 0.02)

    n_pages = PAST_LEN // PAGE_SIZE
    n_pool = (5 * DEC_BATCH * n_pages) // 4
    d = D_MODEL
    inp = {}
    inp['x_prompt'] = nrm((BATCH, SEQ, d), 1.0)
    inp['x_sample'] = nrm((DEC_BATCH, DEC_SEQ, d), 1.0)
    inp['cache_mla'] = nrm((N_EVEN, n_pool, PAGE_SIZE, KV_LORA + MLA_ROPE), 1.0)
    inp['state_conv'] = nrm((N_EVEN, DEC_BATCH, CONV_WIDTH - 1, CONV_CH), 0.5)
    inp['cache_moba_k'] = nrm((N_ODD, n_pool, PAGE_SIZE, MOBA_KV_HEADS, HEAD_DIM), 1.0)
    inp['cache_moba_v'] = nrm((N_ODD, n_pool, PAGE_SIZE, MOBA_KV_HEADS, HEAD_DIM), 1.0)
    perm = jax.random.permutation(nk(), n_pool)[:DEC_BATCH * n_pages]
    inp['page_table'] = perm.reshape(DEC_BATCH, n_pages).astype(jnp.int32)
    inp['c_prompt'] = nrm((BATCH, d), 1.0)
    inp['c_sample'] = nrm((DEC_BATCH, d), 1.0)
    inp['ada_w'] = nrm((DEPTH, d, 6 * d), 0.5 * d ** -0.5)
    inp['ada_b'] = nrm((DEPTH, 6 * d), 0.02)
    inp['norm_mix_g'] = gain((DEPTH, d))
    inp['norm_ffn_g'] = gain((DEPTH, d))
    inp['final_g'] = gain((d,))
    inp['ev_w_in'] = nrm((N_EVEN, d, EVEN_IN), d ** -0.5)
    inp['ev_g_q'] = gain((N_EVEN, Q_LORA))
    inp['ev_w_qb'] = nrm((N_EVEN, Q_LORA, MLA_HEADS * MLA_QK), Q_LORA ** -0.5)
    inp['ev_g_kv'] = gain((N_EVEN, KV_LORA))
    inp['ev_w_kvb'] = nrm((N_EVEN, KV_LORA, MLA_HEADS * (MLA_NOPE + MLA_V)), KV_LORA ** -0.5)
    inp['ev_conv_w'] = nrm((N_EVEN, CONV_WIDTH, CONV_CH), CONV_WIDTH ** -0.5)
    inp['ev_conv_b'] = nrm((N_EVEN, CONV_CH), 0.02)
    inp['ev_ln_g'] = gain((N_EVEN, CONV_CH))
    inp['ev_ln_b'] = nrm((N_EVEN, CONV_CH), 0.02)
    inp['ev_w_out'] = nrm((N_EVEN, MIX_WIDTH, d), MIX_WIDTH ** -0.5)
    inp['od_w_qkv'] = nrm((N_ODD, d, ODD_IN), d ** -0.5)
    inp['od_w_o'] = nrm((N_ODD, MOBA_HEADS * HEAD_DIM, d), (MOBA_HEADS * HEAD_DIM) ** -0.5)
    inp['moe_w_group'] = nrm((DEPTH, d, N_GROUPS), d ** -0.5)
    inp['moe_b_group'] = nrm((DEPTH, N_GROUPS), 0.01)
    inp['moe_w_sec'] = nrm((DEPTH, d, N_GROUPS * EXPERTS_PER_GROUP), d ** -0.5)
    inp['moe_b_sec'] = nrm((DEPTH, N_GROUPS * EXPERTS_PER_GROUP), 0.01)
    inp['moe_w_gate'] = nrm((DEPTH, N_GROUPS, EXPERTS_PER_GROUP, d, EXPERT_FF), d ** -0.5)
    inp['moe_w_up'] = nrm((DEPTH, N_GROUPS, EXPERTS_PER_GROUP, d, EXPERT_FF), d ** -0.5)
    inp['moe_w_down'] = nrm((DEPTH, N_GROUPS, EXPERTS_PER_GROUP, EXPERT_FF, d), EXPERT_FF ** -0.5)
    return inp


def reference(x_prompt, x_sample, cache_mla, state_conv, cache_moba_k, cache_moba_v, page_table,
              c_prompt, c_sample, ada_w, ada_b, norm_mix_g, norm_ffn_g, final_g,
              ev_w_in, ev_g_q, ev_w_qb, ev_g_kv, ev_w_kvb, ev_conv_w, ev_conv_b, ev_ln_g, ev_ln_b, ev_w_out,
              od_w_qkv, od_w_o, moe_w_group, moe_b_group, moe_w_sec, moe_b_sec,
              moe_w_gate, moe_w_up, moe_w_down):
    seq = x_prompt.shape[1]
    dec_seq = x_sample.shape[1]
    past_len = page_table.shape[1] * PAGE_SIZE
    pos_p = jnp.arange(seq)
    pos_s = past_len + jnp.arange(dec_seq)
    xp, xs = x_prompt, x_sample
    mla_p, mla_s, conv_p, conv_s = [], [], [], []
    k_p, v_p, k_s, v_s = [], [], [], []
    for layer in range(DEPTH):
        sh1p, sc1p, g1p, sh2p, sc2p, g2p = ada_modulation(c_prompt, ada_w[layer], ada_b[layer])
        sh1s, sc1s, g1s, sh2s, sc2s, g2s = ada_modulation(c_sample, ada_w[layer], ada_b[layer])
        hp = modulate(rmsnorm(xp, norm_mix_g[layer]), sh1p, sc1p)
        hs = modulate(rmsnorm(xs, norm_mix_g[layer]), sh1s, sc1s)
        i = layer // 2
        if layer % 2 == 0:
            ew = (ev_w_in[i], ev_g_q[i], ev_w_qb[i], ev_g_kv[i], ev_w_kvb[i],
                  ev_conv_w[i], ev_conv_b[i], ev_ln_g[i], ev_ln_b[i], ev_w_out[i])
            op, rows_p, cst_p = even_mixer_prompt(hp, pos_p, *ew)
            osm, rows_s, cst_s = even_mixer_sample(hs, pos_s, cache_mla, state_conv[i], i, page_table, *ew)
            mla_p.append(rows_p)
            mla_s.append(rows_s)
            conv_p.append(cst_p)
            conv_s.append(cst_s)
        else:
            op, kp_new, vp_new = odd_mixer_prompt(hp, pos_p, od_w_qkv[i], od_w_o[i])
            osm, ks_new, vs_new = odd_mixer_sample(hs, pos_s, cache_moba_k, cache_moba_v, i, page_table,
                                                   od_w_qkv[i], od_w_o[i])
            k_p.append(kp_new)
            v_p.append(vp_new)
            k_s.append(ks_new)
            v_s.append(vs_new)
        xp = xp + g1p[:, None, :] * op
        xs = xs + g1s[:, None, :] * osm
        mw = (moe_w_group[layer], moe_b_group[layer], moe_w_sec[layer], moe_b_sec[layer],
              moe_w_gate[layer], moe_w_up[layer], moe_w_down[layer])
        xp = xp + g2p[:, None, :] * hier_moe(modulate(rmsnorm(xp, norm_ffn_g[layer]), sh2p, sc2p), *mw)
        xs = xs + g2s[:, None, :] * hier_moe(modulate(rmsnorm(xs, norm_ffn_g[layer]), sh2s, sc2s), *mw)
    y_prompt = rmsnorm(xp, final_g)
    y_sample = rmsnorm(xs, final_g)
    return (y_prompt, y_sample, jnp.stack(mla_p), jnp.stack(mla_s), jnp.stack(conv_p), jnp.stack(conv_s),
            jnp.stack(k_p), jnp.stack(v_p), jnp.stack(k_s), jnp.stack(v_s))
```

```python
import functools

import jax
import jax.numpy as jnp
from jax import lax
from jax.experimental import pallas as pl
from jax.experimental.pallas import tpu as pltpu

F32 = jnp.float32
BF16 = jnp.bfloat16
I32 = jnp.int32

PAGE = 128
MLA_HEADS = 8
MLA_NOPE = 128
MLA_ROPE = 64
MLA_V = 128
MLA_QK = MLA_NOPE + MLA_ROPE
Q_LORA = 512
KV_LORA = 512
CONV_CH = 1024
CONV_WIDTH = 31
MOBA_HEADS = 16
MOBA_KV_HEADS = 2
MOBA_GROUP = MOBA_HEADS // MOBA_KV_HEADS
HEAD_DIM = 128
MOBA_BLOCK = 256
MOBA_TOPK = 3
N_GROUPS = 4
EXPERTS_PER_GROUP = 8
N_EXPERTS = N_GROUPS * EXPERTS_PER_GROUP
EXPERT_FF = 512
ROPE_THETA = 10000.0
EPS = 1e-6

LANES = 128
ROW_TILE = 128
VMEM_LIMIT = 56 * 1024 * 1024
NEG_BIG = -1e30


def _params(sem, vmem=VMEM_LIMIT):
    return pltpu.CompilerParams(dimension_semantics=sem, vmem_limit_bytes=vmem)


def _ada_kernel(c_ref, w_ref, b_ref, o_ref):
    c = c_ref[...]
    sc = (c * jax.nn.sigmoid(c)).astype(BF16)
    o_ref[0] = jnp.dot(sc, w_ref[0].astype(BF16), preferred_element_type=F32) + b_ref[0]


def ada_modulation(c_all, ada_w, ada_b, tn=1024):
    depth, d, n = ada_w.shape
    rows = c_all.shape[0]
    return pl.pallas_call(
        _ada_kernel,
        grid=(depth, n // tn),
        in_specs=[pl.BlockSpec((rows, d), lambda l, j: (0, 0)),
                  pl.BlockSpec((1, d, tn), lambda l, j: (l, 0, j)),
                  pl.BlockSpec((1, 1, tn), lambda l, j: (l, 0, j))],
        out_specs=pl.BlockSpec((1, rows, tn), lambda l, j: (l, 0, j)),
        out_shape=jax.ShapeDtypeStruct((depth, rows, n), F32),
        compiler_params=_params(("arbitrary", "arbitrary")),
        name="ada_modulation",
    )(c_all, ada_w, ada_b.reshape(depth, 1, n))


def _mod_rows(ref, i, n_prompt_tiles, tiles_per_batch, n_batch, n_sample):
    is_prompt = i < n_prompt_tiles
    prow = n_sample + jnp.minimum(i // tiles_per_batch, n_batch - 1)
    prompt_row = ref[0, pl.ds(prow, 1), :]
    sample_rows = ref[0, 0:ROW_TILE, :]
    return jnp.where(is_prompt, prompt_row, sample_rows)


def _route(logits, carry_ref):
    rows = logits.shape[0]
    lane = lax.broadcasted_iota(I32, (rows, LANES), 1)
    neg_inf = jnp.float32(-jnp.inf)
    gl = jnp.where(lane < N_GROUPS, logits, neg_inf)
    gmax = jnp.max(gl, axis=1, keepdims=True)
    gidx = jnp.min(jnp.where(gl == gmax, lane, LANES), axis=1, keepdims=True)
    gsum = jnp.sum(jnp.where(lane < N_GROUPS, jnp.exp(logits - gmax), 0.0), axis=1, keepdims=True)
    gval = 1.0 / gsum
    lo = N_GROUPS + EXPERTS_PER_GROUP * gidx
    el = jnp.where((lane >= lo) & (lane < lo + EXPERTS_PER_GROUP), logits, neg_inf)
    v1 = jnp.max(el, axis=1, keepdims=True)
    i1 = jnp.min(jnp.where(el == v1, lane, LANES), axis=1, keepdims=True)
    el2 = jnp.where(lane == i1, neg_inf, el)
    v2 = jnp.max(el2, axis=1, keepdims=True)
    i2 = jnp.min(jnp.where(el2 == v2, lane, LANES), axis=1, keepdims=True)
    t = jnp.exp(v2 - v1)
    w1 = gval / (1.0 + t)
    w2 = gval * t / (1.0 + t)
    e1 = i1 - N_GROUPS
    e2 = i2 - N_GROUPS
    r = lax.broadcasted_iota(I32, (rows, rows), 0)
    c = lax.broadcasted_iota(I32, (rows, rows), 1)
    tri = jnp.where(c < r, 1.0, 0.0).astype(BF16)
    carry = carry_ref[...]
    oh1 = jnp.where(lane == e1, 1.0, 0.0)
    oh2 = jnp.where(lane == e2, 1.0, 0.0)
    ex1 = jnp.dot(tri, oh1.astype(BF16), preferred_element_type=F32) + carry
    carry = carry + jnp.sum(oh1, axis=0, keepdims=True)
    ex2 = jnp.dot(tri, oh2.astype(BF16), preferred_element_type=F32) + carry
    carry = carry + jnp.sum(oh2, axis=0, keepdims=True)
    carry_ref[...] = carry
    rank1 = jnp.sum(oh1 * ex1, axis=1, keepdims=True)
    rank2 = jnp.sum(oh2 * ex2, axis=1, keepdims=True)
    slab = jnp.where(lane == 0, e1.astype(F32),
           jnp.where(lane == 1, e2.astype(F32),
           jnp.where(lane == 2, w1,
           jnp.where(lane == 3, w2,
           jnp.where(lane == 4, rank1,
           jnp.where(lane == 5, rank2, 0.0))))))
    return slab, carry


def _prenorm_kernel(*refs, has_o, modulate, router, geom):
    refs = list(refs)
    x_ref = refs.pop(0)
    o_ref = gate_ref = shift_ref = scale_ref = wr_ref = br_ref = None
    if has_o:
        o_ref = refs.pop(0)
        gate_ref = refs.pop(0)
    if modulate:
        shift_ref = refs.pop(0)
        scale_ref = refs.pop(0)
    g_ref = refs.pop(0)
    if router:
        wr_ref = refs.pop(0)
        br_ref = refs.pop(0)
    xn_ref = refs.pop(0) if has_o else None
    h_ref = refs.pop(0)
    if router:
        slab_ref = refs.pop(0)
        cnt_ref = refs.pop(0)
        carry_ref = refs.pop(0)

    i = pl.program_id(0)
    x = x_ref[...]
    if has_o:
        x = x + _mod_rows(gate_ref, i, *geom) * o_ref[...].astype(F32)
        xn_ref[...] = x
    y = x * lax.rsqrt(jnp.mean(x * x, axis=-1, keepdims=True) + EPS) * g_ref[...]
    if modulate:
        y = y * (1.0 + _mod_rows(scale_ref, i, *geom)) + _mod_rows(shift_ref, i, *geom)
    h_ref[...] = y.astype(h_ref.dtype)
    if router:
        @pl.when(i == 0)
        def _():
            carry_ref[...] = jnp.zeros_like(carry_ref)
        logits = jnp.dot(y, wr_ref[...], precision=lax.Precision.HIGHEST,
                         preferred_element_type=F32) + br_ref[...]
        slab, carry = _route(logits, carry_ref)
        slab_ref[...] = slab
        cnt_ref[...] = carry


def resid_prenorm(x, g, geom, *, o=None, mod=None, gate=None, shift=None, scale=None, router=None, h_dtype=F32):
    t, d = x.shape
    has_o = o is not None
    modulate = shift is not None
    n_tiles = t // ROW_TILE
    row_spec = pl.BlockSpec((ROW_TILE, d), lambda i: (i, 0))

    def mod_spec(where):
        layer, k = where
        return pl.BlockSpec((1, mod.shape[1], d), lambda i: (layer, 0, k))

    args, in_specs = [x], [row_spec]
    if has_o:
        args += [o, mod]
        in_specs += [row_spec, mod_spec(gate)]
    if modulate:
        args += [mod, mod]
        in_specs += [mod_spec(shift), mod_spec(scale)]
    args.append(g.reshape(1, d))
    in_specs.append(pl.BlockSpec((1, d), lambda i: (0, 0)))
    out_shape, out_specs, scratch = [], [], []
    if has_o:
        out_shape.append(jax.ShapeDtypeStruct((t, d), F32))
        out_specs.append(row_spec)
    out_shape.append(jax.ShapeDtypeStruct((t, d), h_dtype))
    out_specs.append(row_spec)
    if router is not None:
        w_r, b_r = router
        args += [w_r, b_r]
        in_specs += [pl.BlockSpec((d, LANES), lambda i: (0, 0)), pl.BlockSpec((1, LANES), lambda i: (0, 0))]
        out_shape += [jax.ShapeDtypeStruct((t, LANES), F32), jax.ShapeDtypeStruct((1, LANES), F32)]
        out_specs += [pl.BlockSpec((ROW_TILE, LANES), lambda i: (i, 0)), pl.BlockSpec((1, LANES), lambda i: (0, 0))]
        scratch.append(pltpu.VMEM((1, LANES), F32))
    kern = functools.partial(_prenorm_kernel, has_o=has_o, modulate=modulate, router=router is not None, geom=geom)
    return pl.pallas_call(
        kern, grid=(n_tiles,), in_specs=in_specs, out_specs=out_specs, out_shape=out_shape,
        scratch_shapes=scratch, compiler_params=_params(("arbitrary",)), name="resid_prenorm",
    )(*args)


def _roll_is_backward():
    lane = lax.broadcasted_iota(I32, (8, LANES), 1)
    return pltpu.roll(lane, 32, 1) == ((lane + LANES - 32) % LANES)


def _rope_group(blk, cos, sin_signed, half):
    if 2 * half == LANES:
        rot = pltpu.roll(blk, half, 1)
    else:
        lane = lax.broadcasted_iota(I32, blk.shape, 1)
        ra = pltpu.roll(blk, half, 1)
        rb = pltpu.roll(blk, LANES - half, 1)
        back = _roll_is_backward()[0:1, :]
        minus = jnp.where(back, ra, rb)
        plus = jnp.where(back, rb, ra)
        rot = jnp.where(lane % (2 * half) < half, plus, minus)
    return blk * cos + rot * sin_signed


def _mm_kernel(*refs, rope_half, rope_lo, rope_hi, tn):
    if rope_half:
        x_ref, w_ref, cos_ref, sin_ref, o_ref, wbf_ref = refs
    else:
        x_ref, w_ref, o_ref, wbf_ref = refs
    j = pl.program_id(0)
    i = pl.program_id(1)

    @pl.when(i == 0)
    def _():
        wbf_ref[...] = w_ref[...].astype(BF16)

    acc = jnp.dot(x_ref[...].astype(BF16), wbf_ref[...], preferred_element_type=F32)
    if rope_half:
        cos = cos_ref[...]
        sin = sin_ref[...]
        parts = []
        for gidx in range(tn // LANES):
            blk = acc[:, gidx * LANES:(gidx + 1) * LANES]
            col = j * tn + gidx * LANES
            roped = _rope_group(blk, cos, sin, rope_half)
            parts.append(jnp.where((col >= rope_lo) & (col < rope_hi), roped, blk))
        acc = jnp.concatenate(parts, axis=1)
    o_ref[...] = acc.astype(o_ref.dtype)


def matmul(x, w, *, tm, tn, out_dtype=F32, w_index=None, rope=None, row_blocks=None):
    m, k = x.shape
    n = w.shape[-1]
    mt = row_blocks if row_blocks is not None else pl.cdiv(m, tm)
    if w.ndim == 3:
        w_spec = pl.BlockSpec((None, k, tn), lambda j, i: (w_index, 0, j))
    else:
        w_spec = pl.BlockSpec((k, tn), lambda j, i: (0, j))
    args = [x, w]
    in_specs = [pl.BlockSpec((tm, k), lambda j, i: (i, 0)), w_spec]
    rope_half = rope_lo = rope_hi = 0
    if rope is not None:
        rope_half, rope_lo, rope_hi, cos, sin = rope
        args += [cos, sin]
        in_specs += [pl.BlockSpec((tm, LANES), lambda j, i: (i, 0))] * 2
    kern = functools.partial(_mm_kernel, rope_half=rope_half, rope_lo=rope_lo, rope_hi=rope_hi, tn=tn)
    return pl.pallas_call(
        kern, grid=(pl.cdiv(n, tn), mt), in_specs=in_specs,
        out_specs=pl.BlockSpec((tm, tn), lambda j, i: (i, j)),
        out_shape=jax.ShapeDtypeStruct((mt * tm if row_blocks is not None else m, n), out_dtype),
        scratch_shapes=[pltpu.VMEM((k, tn), BF16)],
        compiler_params=_params(("arbitrary", "arbitrary")), name="matmul",
    )(*args)


def _even_post_kernel(proj_ref, kpe_ref, gq_ref, gkv_ref, cos_ref, sin_ref,
                      cqn_ref, ckvn_ref, rows_ref, kpe_out_ref, u_ref):
    def rms(v, g):
        return v * lax.rsqrt(jnp.mean(v * v, axis=-1, keepdims=True) + EPS) * g

    cq = proj_ref[:, 0:Q_LORA]
    ckv = proj_ref[:, Q_LORA:Q_LORA + KV_LORA]
    ga = proj_ref[:, Q_LORA + KV_LORA:Q_LORA + KV_LORA + CONV_CH]
    gb = proj_ref[:, Q_LORA + KV_LORA + CONV_CH:Q_LORA + KV_LORA + 2 * CONV_CH]
    cqn_ref[...] = rms(cq, gq_ref[...]).astype(cqn_ref.dtype)
    c_kv = rms(ckv, gkv_ref[...])
    ckvn_ref[...] = c_kv.astype(ckvn_ref.dtype)
    k_pe = _rope_group(kpe_ref[...], cos_ref[...], sin_ref[...], MLA_ROPE // 2)
    kpe_out_ref[...] = k_pe.astype(kpe_out_ref.dtype)
    rows_ref[:, 0:KV_LORA] = c_kv
    rows_ref[:, KV_LORA:KV_LORA + MLA_ROPE] = k_pe[:, 0:MLA_ROPE]
    u_ref[...] = ga * jax.nn.sigmoid(gb)


def even_post(proj, kpe, g_q, g_kv, cos64, sin64, tm=256):
    t = proj.shape[0]
    width = proj.shape[1]
    rs = lambda w: pl.BlockSpec((tm, w), lambda i: (i, 0))
    vs = lambda w: pl.BlockSpec((1, w), lambda i: (0, 0))
    return pl.pallas_call(
        _even_post_kernel, grid=(pl.cdiv(t, tm),),
        in_specs=[rs(width), rs(LANES), vs(Q_LORA), vs(KV_LORA), rs(LANES), rs(LANES)],
        out_specs=[rs(Q_LORA), rs(KV_LORA), rs(KV_LORA + MLA_ROPE), rs(LANES), rs(CONV_CH)],
        out_shape=[jax.ShapeDtypeStruct((t, Q_LORA), BF16), jax.ShapeDtypeStruct((t, KV_LORA), BF16),
                   jax.ShapeDtypeStruct((t, KV_LORA + MLA_ROPE), F32), jax.ShapeDtypeStruct((t, LANES), BF16),
                   jax.ShapeDtypeStruct((t, CONV_CH), F32)],
        compiler_params=_params(("arbitrary",)), name="even_post",
    )(proj, kpe, g_q.reshape(1, -1), g_kv.reshape(1, -1), cos64, sin64)


def _mla_flash_kernel(qn_ref, qp_ref, kn_ref, kp_ref, v_ref, o_ref, *, tq, scale):
    qi = pl.program_id(2)
    q = jnp.concatenate([qn_ref[...], qp_ref[...]], axis=1)

    def scores(kj):
        ks = pl.multiple_of(kj * tq, tq)
        k = jnp.concatenate([kn_ref[pl.ds(ks, tq), :], kp_ref[pl.ds(ks, tq), :]], axis=1)
        s = lax.dot_general(q, k, (((1,), (1,)), ((), ())), preferred_element_type=F32) * scale
        return s, v_ref[pl.ds(ks, tq), :]

    s, v = scores(qi)
    row = lax.broadcasted_iota(I32, (tq, tq), 0)
    col = lax.broadcasted_iota(I32, (tq, tq), 1)
    s = jnp.where(col <= row, s, -jnp.inf)
    m = jnp.max(s, axis=1, keepdims=True)
    p = jnp.exp(s - m)
    l = jnp.sum(p, axis=1, keepdims=True)
    acc = jnp.dot(p.astype(BF16), v, preferred_element_type=F32)

    def body(kj, carry):
        m, l, acc = carry
        s, v = scores(kj)
        m_new = jnp.maximum(m, jnp.max(s, axis=1, keepdims=True))
        corr = jnp.exp(m - m_new)
        p = jnp.exp(s - m_new)
        return (m_new, l * corr + jnp.sum(p, axis=1, keepdims=True),
                acc * corr + jnp.dot(p.astype(BF16), v, preferred_element_type=F32))

    m, l, acc = lax.fori_loop(0, qi, body, (m, l, acc))
    o_ref[...] = (acc / l).astype(o_ref.dtype)


def mla_flash(q, kv, kpe, n_batch, seq, tq=256):
    nq = seq // tq
    kern = functools.partial(_mla_flash_kernel, tq=tq, scale=MLA_QK ** -0.5)
    return pl.pallas_call(
        kern, grid=(n_batch, MLA_HEADS, nq),
        in_specs=[pl.BlockSpec((tq, LANES), lambda b, h, i: (b * nq + i, h)),
                  pl.BlockSpec((tq, LANES), lambda b, h, i: (b * nq + i, MLA_HEADS + h)),
                  pl.BlockSpec((seq, LANES), lambda b, h, i: (b, 2 * h)),
                  pl.BlockSpec((seq, LANES), lambda b, h, i: (b, 0)),
                  pl.BlockSpec((seq, LANES), lambda b, h, i: (b, 2 * h + 1))],
        out_specs=pl.BlockSpec((tq, LANES), lambda b, h, i: (b * nq + i, h)),
        out_shape=jax.ShapeDtypeStruct((n_batch * seq, MLA_HEADS * MLA_V), BF16),
        compiler_params=_params(("arbitrary", "arbitrary", "arbitrary")), name="mla_flash",
    )(q, q, kv, kpe, kv)


def _absorb_q_kernel(q_ref, w_ref, o_ref):
    o_ref[0] = lax.dot_general(q_ref[...], w_ref[...].astype(BF16), (((1,), (1,)), ((), ())),
                               preferred_element_type=F32)


def absorb_q(q, w_kvb, layer, row_block):
    return pl.pallas_call(
        _absorb_q_kernel, grid=(MLA_HEADS,),
        in_specs=[pl.BlockSpec((ROW_TILE, LANES), lambda h: (row_block, h)),
                  pl.BlockSpec((None, KV_LORA, LANES), lambda h: (layer, 0, 2 * h))],
        out_specs=pl.BlockSpec((1, ROW_TILE, KV_LORA), lambda h: (h, 0, 0)),
        out_shape=jax.ShapeDtypeStruct((MLA_HEADS, ROW_TILE, KV_LORA), F32),
        compiler_params=_params(("arbitrary",)), name="absorb_q",
    )(q, w_kvb)


def _absorb_v_kernel(o_ref_in, w_ref, out_ref):
    out_ref[...] = jnp.dot(o_ref_in[0].astype(BF16), w_ref[...].astype(BF16),
                           preferred_element_type=F32).astype(out_ref.dtype)


def absorb_v(o_lat, w_kvb, layer):
    return pl.pallas_call(
        _absorb_v_kernel, grid=(MLA_HEADS,),
        in_specs=[pl.BlockSpec((1, ROW_TILE, KV_LORA), lambda h: (h, 0, 0)),
                  pl.BlockSpec((None, KV_LORA, LANES), lambda h: (layer, 0, 2 * h + 1))],
        out_specs=pl.BlockSpec((ROW_TILE, LANES), lambda h: (0, h)),
        out_shape=jax.ShapeDtypeStruct((ROW_TILE, MLA_HEADS * MLA_V), BF16),
        compiler_params=_params(("arbitrary",)), name="absorb_v",
    )(o_lat, w_kvb)


def _mla_decode_kernel(pt_ref, qlat_ref, qpe_ref, new_ref, cache_ref, o_ref, buf_ref, sem_ref,
                       *, layer, n_pages, group, scale):
    n = pl.program_id(0)
    n_seq = pl.num_programs(0)
    iters = n_pages // group
    total = n_seq * iters

    def copies(it, slot):
        return [pltpu.make_async_copy(cache_ref.at[layer, pt_ref[it * group + g]],
                                      buf_ref.at[slot, g], sem_ref.at[slot]) for g in range(group)]

    @pl.when(n == 0)
    def _():
        for cp in copies(0, 0):
            cp.start()

    qlat = qlat_ref[0].astype(BF16)
    qpe = qpe_ref[0][:, 0:MLA_ROPE].astype(BF16)
    heads = qlat.shape[0]

    def body(j, carry):
        m, l, acc = carry
        it = n * iters + j
        slot = it % 2
        for cp in copies(it, slot):
            cp.wait()

        @pl.when(it + 1 < total)
        def _():
            for cp in copies(it + 1, 1 - slot):
                cp.start()

        rows = buf_ref[slot].reshape(group * PAGE, KV_LORA + MLA_ROPE).astype(BF16)
        lat = rows[:, 0:KV_LORA]
        s = (lax.dot_general(qlat, lat, (((1,), (1,)), ((), ())), preferred_element_type=F32)
             + lax.dot_general(qpe, rows[:, KV_LORA:], (((1,), (1,)), ((), ())), preferred_element_type=F32)) * scale
        m_new = jnp.maximum(m, jnp.max(s, axis=1, keepdims=True))
        corr = jnp.exp(m - m_new)
        p = jnp.exp(s - m_new)
        return (m_new, l * corr + jnp.sum(p, axis=1, keepdims=True),
                acc * corr + jnp.dot(p.astype(BF16), lat, preferred_element_type=F32))

    init = (jnp.full((heads, 1), -jnp.inf, F32), jnp.zeros((heads, 1), F32), jnp.zeros((heads, KV_LORA), F32))
    m, l, acc = lax.fori_loop(0, iters, body, init)
    new = new_ref[0]
    s = (jnp.sum(qlat_ref[0] * new[:, 0:KV_LORA], axis=1, keepdims=True)
         + jnp.sum(qpe_ref[0][:, 0:MLA_ROPE] * new[:, KV_LORA:], axis=1, keepdims=True)) * scale
    m_new = jnp.maximum(m, s)
    corr = jnp.exp(m - m_new)
    p = jnp.exp(s - m_new)
    l = l * corr + p
    acc = acc * corr + p * new[:, 0:KV_LORA]
    o_ref[0] = acc / l


def mla_decode(q_lat, q_pe, new_rows, cache, page_table, layer, group=2):
    n_seq, heads, _ = q_lat.shape
    n_pages = page_table.shape[1]
    width = KV_LORA + MLA_ROPE
    kern = functools.partial(_mla_decode_kernel, layer=layer, n_pages=n_pages, group=group, scale=MLA_QK ** -0.5)
    grid_spec = pltpu.PrefetchScalarGridSpec(
        num_scalar_prefetch=1, grid=(n_seq,),
        in_specs=[pl.BlockSpec((1, heads, KV_LORA), lambda n, pt: (n, 0, 0)),
                  pl.BlockSpec((1, heads, LANES), lambda n, pt: (n, 0, 0)),
                  pl.BlockSpec((1, 1, width), lambda n, pt: (n, 0, 0)),
                  pl.BlockSpec(memory_space=pl.ANY)],
        out_specs=pl.BlockSpec((1, heads, KV_LORA), lambda n, pt: (n, 0, 0)),
        scratch_shapes=[pltpu.VMEM((2, group, PAGE, width), F32), pltpu.SemaphoreType.DMA((2,))])
    return pl.pallas_call(
        kern, grid_spec=grid_spec, out_shape=jax.ShapeDtypeStruct((n_seq, heads, KV_LORA), F32),
        compiler_params=_params(("arbitrary",)), name="mla_decode",
    )(page_table.reshape(-1), q_lat, q_pe, new_rows, cache)


def _ln_swish(y, g, b):
    mu = jnp.mean(y, axis=-1, keepdims=True)
    var = jnp.mean(jnp.square(y - mu), axis=-1, keepdims=True)
    z = (y - mu) * lax.rsqrt(var + EPS) * g + b
    return z * jax.nn.sigmoid(z)


def _conv_prompt_kernel(prev_ref, cur_ref, w_ref, cb_ref, g_ref, b_ref, o_ref, win_ref, y_ref, *, ts, tiles_per_seq):
    i = pl.program_id(0)
    first = (i % tiles_per_seq) == 0
    win_ref[0:ts, :] = jnp.where(first, 0.0, prev_ref[...])
    win_ref[ts:2 * ts, :] = cur_ref[...]
    rc = 64
    for c in range(CONV_CH // LANES):
        cs = slice(c * LANES, (c + 1) * LANES)
        for r0 in range(0, ts, rc):
            acc = jnp.zeros((rc, LANES), F32)
            for k in range(CONV_WIDTH):
                lo = ts - (CONV_WIDTH - 1) + k + r0
                acc = acc + w_ref[k:k + 1, cs] * win_ref[lo:lo + rc, cs]
            y_ref[r0:r0 + rc, cs] = acc
    y = y_ref[...] + cb_ref[...]
    o_ref[...] = _ln_swish(y, g_ref[...], b_ref[...]).astype(o_ref.dtype)


def conv_prompt(u, conv_w, conv_b, ln_g, ln_b, n_rows, seq, ts=256):
    c = u.shape[1]
    tiles_per_seq = seq // ts
    vs = pl.BlockSpec((1, c), lambda i: (0, 0))
    kern = functools.partial(_conv_prompt_kernel, ts=ts, tiles_per_seq=tiles_per_seq)
    return pl.pallas_call(
        kern, grid=(n_rows // ts,),
        in_specs=[pl.BlockSpec((ts, c), lambda i: (jnp.maximum(i - 1, 0), 0)),
                  pl.BlockSpec((ts, c), lambda i: (i, 0)),
                  pl.BlockSpec((CONV_WIDTH, c), lambda i: (0, 0)), vs, vs, vs],
        out_specs=pl.BlockSpec((ts, c), lambda i: (i, 0)),
        out_shape=jax.ShapeDtypeStruct((n_rows, c), BF16),
        scratch_shapes=[pltpu.VMEM((2 * ts, c), F32), pltpu.VMEM((ts, c), F32)],
        compiler_params=_params(("arbitrary",)), name="conv_prompt",
    )(u, u, conv_w, conv_b.reshape(1, c), ln_g.reshape(1, c), ln_b.reshape(1, c))


def _conv_sample_kernel(win_ref, w_ref, cb_ref, g_ref, b_ref, o_ref):
    y = jnp.sum(win_ref[...] * w_ref[...][None], axis=1) + cb_ref[...]
    o_ref[...] = _ln_swish(y, g_ref[...], b_ref[...]).astype(o_ref.dtype)


def conv_sample(win, conv_w, conv_b, ln_g, ln_b, ns=8):
    n, _, c = win.shape
    vs = pl.BlockSpec((1, c), lambda i: (0, 0))
    return pl.pallas_call(
        _conv_sample_kernel, grid=(n // ns,),
        in_specs=[pl.BlockSpec((ns, CONV_WIDTH, c), lambda i: (i, 0, 0)),
                  pl.BlockSpec((CONV_WIDTH, c), lambda i: (0, 0)), vs, vs, vs],
        out_specs=pl.BlockSpec((ns, c), lambda i: (i, 0)),
        out_shape=jax.ShapeDtypeStruct((n, c), BF16),
        compiler_params=_params(("arbitrary",)), name="conv_sample",
    )(win, conv_w, conv_b.reshape(1, c), ln_g.reshape(1, c), ln_b.reshape(1, c))


def _moba_prompt_kernel(q_ref, k_ref, v_ref, o_ref, *, n_blk, scale):
    qi = pl.program_id(2)
    blk = MOBA_BLOCK
    q = jnp.concatenate([q_ref[:, h * HEAD_DIM:(h + 1) * HEAD_DIM] for h in range(MOBA_GROUP)], axis=0)
    rows = q.shape[0]
    qb = q.astype(BF16)

    kmean = jnp.concatenate(
        [jnp.mean(k_ref[j * blk:(j + 1) * blk, :], axis=0, keepdims=True) for j in range(n_blk)]
        + [jnp.zeros((LANES - n_blk, HEAD_DIM), F32)], axis=0)
    gate = lax.dot_general(q, kmean, (((1,), (1,)), ((), ())), precision=lax.Precision.HIGHEST,
                           preferred_element_type=F32)
    lane = lax.broadcasted_iota(I32, (rows, LANES), 1)
    cand = jnp.where(lane < qi, gate, -jnp.inf)
    sel = jnp.zeros((rows, LANES), F32)
    for _ in range(MOBA_TOPK):
        top = jnp.max(cand, axis=1, keepdims=True)
        idx = jnp.min(jnp.where(cand == top, lane, LANES), axis=1, keepdims=True)
        pick = (lane == idx) & (lane < qi)
        sel = jnp.where(pick, 1.0, sel)
        cand = jnp.where(pick, -jnp.inf, cand)
    bias = jnp.where(sel > 0.0, 0.0, NEG_BIG)

    def tile(j):
        ks = pl.multiple_of(j * blk, blk)
        k = k_ref[pl.ds(ks, blk), :].astype(BF16)
        s = lax.dot_general(qb, k, (((1,), (1,)), ((), ())), preferred_element_type=F32) * scale
        return s, v_ref[pl.ds(ks, blk), :].astype(BF16)

    s, v = tile(qi)
    pos = lax.broadcasted_iota(I32, (rows, blk), 0) % blk
    col = lax.broadcasted_iota(I32, (rows, blk), 1)
    s = jnp.where(col <= pos, s, -jnp.inf)
    m = jnp.max(s, axis=1, keepdims=True)
    p = jnp.exp(s - m)
    l = jnp.sum(p, axis=1, keepdims=True)
    acc = jnp.dot(p.astype(BF16), v, preferred_element_type=F32)

    def body(j, carry):
        m, l, acc = carry
        s, v = tile(j)
        s = s + jnp.sum(jnp.where(lane == j, bias, 0.0), axis=1, keepdims=True)
        m_new = jnp.maximum(m, jnp.max(s, axis=1, keepdims=True))
        corr = jnp.exp(m - m_new)
        p = jnp.exp(s - m_new)
        return (m_new, l * corr + jnp.sum(p, axis=1, keepdims=True),
                acc * corr + jnp.dot(p.astype(BF16), v, preferred_element_type=F32))

    m, l, acc = lax.fori_loop(0, qi, body, (m, l, acc))
    out = (acc / l).astype(o_ref.dtype)
    for h in range(MOBA_GROUP):
        o_ref[:, h * HEAD_DIM:(h + 1) * HEAD_DIM] = out[h * blk:(h + 1) * blk, :]


def moba_prompt(qkv, n_batch, seq):
    n_blk = seq // MOBA_BLOCK
    gw = MOBA_GROUP * HEAD_DIM
    kcol = MOBA_HEADS
    kern = functools.partial(_moba_prompt_kernel, n_blk=n_blk, scale=HEAD_DIM ** -0.5)
    return pl.pallas_call(
        kern, grid=(n_batch, MOBA_KV_HEADS, n_blk),
        in_specs=[pl.BlockSpec((MOBA_BLOCK, gw), lambda b, g, i: (b * n_blk + i, g)),
                  pl.BlockSpec((seq, HEAD_DIM), lambda b, g, i: (b, kcol + g)),
                  pl.BlockSpec((seq, HEAD_DIM), lambda b, g, i: (b, kcol + MOBA_KV_HEADS + g))],
        out_specs=pl.BlockSpec((MOBA_BLOCK, gw), lambda b, g, i: (b * n_blk + i, g)),
        out_shape=jax.ShapeDtypeStruct((n_batch * seq, MOBA_HEADS * HEAD_DIM), BF16),
        compiler_params=_params(("arbitrary", "arbitrary", "arbitrary")), name="moba_prompt",
    )(qkv, qkv, qkv)


def _moba_gate_kernel(pt_ref, q_ref, cache_ref, idx_ref, buf_ref, mean_ref, sem_ref, *, layer, n_pages):
    n = pl.program_id(0)
    n_seq = pl.num_programs(0)
    ppb = MOBA_BLOCK // PAGE
    n_fb = n_pages // ppb
    total = n_seq * n_fb

    def copies(it, slot):
        out = []
        for g in range(ppb):
            for kvh in range(MOBA_KV_HEADS):
                out.append(pltpu.make_async_copy(cache_ref.at[layer, pt_ref[it * ppb + g], :, kvh, :],
                                                 buf_ref.at[slot, kvh, g], sem_ref.at[slot]))
        return out

    @pl.when(n == 0)
    def _():
        for cp in copies(0, 0):
            cp.start()

    def body(j, _):
        it = n * n_fb + j
        slot = it % 2
        for cp in copies(it, slot):
            cp.wait()

        @pl.when(it + 1 < total)
        def _():
            for cp in copies(it + 1, 1 - slot):
                cp.start()

        for kvh in range(MOBA_KV_HEADS):
            page_means = [jnp.mean(buf_ref[slot, kvh, g], axis=0, keepdims=True) for g in range(ppb)]
            mean_ref[kvh, pl.ds(j, 1), :] = sum(page_means) / float(ppb)
        return 0

    lax.fori_loop(0, n_fb, body, 0)
    q = q_ref[0]
    rows = MOBA_GROUP
    lane = lax.broadcasted_iota(I32, (rows, LANES), 1)
    out = []
    for kvh in range(MOBA_KV_HEADS):
        means = jnp.concatenate([mean_ref[kvh], jnp.zeros((LANES - n_fb, HEAD_DIM), F32)], axis=0)
        gate = lax.dot_general(q[kvh * rows:(kvh + 1) * rows, :], means, (((1,), (1,)), ((), ())),
                               precision=lax.Precision.HIGHEST, preferred_element_type=F32)
        cand = jnp.where(lane < n_fb, gate, -jnp.inf)
        res = jnp.zeros((rows, LANES), I32)
        for c in range(MOBA_TOPK):
            top = jnp.max(cand, axis=1, keepdims=True)
            idx = jnp.min(jnp.where(cand == top, lane, LANES), axis=1, keepdims=True)
            res = jnp.where(lane == c, idx, res)
            cand = jnp.where(lane == idx, -jnp.inf, cand)
        out.append(res)
    idx_ref[0] = jnp.concatenate(out, axis=0)


def moba_gate(q_s, cache_k, page_table, layer):
    n_seq = q_s.shape[0]
    n_pages = page_table.shape[1]
    ppb = MOBA_BLOCK // PAGE
    n_fb = n_pages // ppb
    kern = functools.partial(_moba_gate_kernel, layer=layer, n_pages=n_pages)
    grid_spec = pltpu.PrefetchScalarGridSpec(
        num_scalar_prefetch=1, grid=(n_seq,),
        in_specs=[pl.BlockSpec((1, MOBA_HEADS, HEAD_DIM), lambda n, pt: (n, 0, 0)),
                  pl.BlockSpec(memory_space=pl.ANY)],
        out_specs=pl.BlockSpec((1, MOBA_HEADS, LANES), lambda n, pt: (n, 0, 0)),
        scratch_shapes=[pltpu.VMEM((2, MOBA_KV_HEADS, ppb, PAGE, HEAD_DIM), F32),
                        pltpu.VMEM((MOBA_KV_HEADS, n_fb, HEAD_DIM), F32),
                        pltpu.SemaphoreType.DMA((2,))])
    return pl.pallas_call(
        kern, grid_spec=grid_spec, out_shape=jax.ShapeDtypeStruct((n_seq, MOBA_HEADS, LANES), I32),
        compiler_params=_params(("arbitrary",)), name="moba_gate",
    )(page_table.reshape(-1), q_s, cache_k)


def _moba_decode_kernel(pt_ref, idx_ref, q_ref, kn_ref, vn_ref, ck_ref, cv_ref, o_ref, kbuf_ref, vbuf_ref, sem_ref,
                        *, layer, n_pages, scale):
    n = pl.program_id(0)
    ppb = MOBA_BLOCK // PAGE

    def copies(h):
        kvh = h // MOBA_GROUP
        out = []
        for c in range(MOBA_TOPK):
            blk = idx_ref[(n * MOBA_HEADS + h) * MOBA_TOPK + c]
            for g in range(ppb):
                page = pt_ref[n * n_pages + blk * ppb + g]
                out.append(pltpu.make_async_copy(ck_ref.at[layer, page, :, kvh, :], kbuf_ref.at[h, c * ppb + g], sem_ref.at[0]))
                out.append(pltpu.make_async_copy(cv_ref.at[layer, page, :, kvh, :], vbuf_ref.at[h, c * ppb + g], sem_ref.at[1]))
        return out

    for h in range(MOBA_HEADS):
        for cp in copies(h):
            cp.start()
    for h in range(MOBA_HEADS):
        for cp in copies(h):
            cp.wait()

    n_keys = MOBA_TOPK * MOBA_BLOCK
    for h in range(MOBA_HEADS):
        kvh = h // MOBA_GROUP
        q = q_ref[0, h:h + 1, :]
        kn = kn_ref[0, kvh:kvh + 1, :]
        vn = vn_ref[0, kvh:kvh + 1, :]
        k = kbuf_ref[h].reshape(n_keys, HEAD_DIM)
        v = vbuf_ref[h].reshape(n_keys, HEAD_DIM)
        s = jnp.sum(k * q, axis=1, keepdims=True) * scale
        s_new = jnp.sum(kn * q, axis=1, keepdims=True) * scale
        m = jnp.maximum(jnp.max(s, axis=0, keepdims=True), s_new)
        p = jnp.exp(s - m)
        p_new = jnp.exp(s_new - m)
        l = jnp.sum(p, axis=0, keepdims=True) + p_new
        o = (jnp.sum(p * v, axis=0, keepdims=True) + p_new * vn) / l
        o_ref[0, h:h + 1, :] = o.astype(o_ref.dtype)


def moba_decode(q_s, k_new, v_new, idx, cache_k, cache_v, page_table, layer):
    n_seq = q_s.shape[0]
    n_pages = page_table.shape[1]
    ppb = MOBA_BLOCK // PAGE
    kern = functools.partial(_moba_decode_kernel, layer=layer, n_pages=n_pages, scale=HEAD_DIM ** -0.5)
    grid_spec = pltpu.PrefetchScalarGridSpec(
        num_scalar_prefetch=2, grid=(n_seq,),
        in_specs=[pl.BlockSpec((1, MOBA_HEADS, HEAD_DIM), lambda n, pt, ix: (n, 0, 0)),
                  pl.BlockSpec((1, MOBA_KV_HEADS, HEAD_DIM), lambda n, pt, ix: (n, 0, 0)),
                  pl.BlockSpec((1, MOBA_KV_HEADS, HEAD_DIM), lambda n, pt, ix: (n, 0, 0)),
                  pl.BlockSpec(memory_space=pl.ANY), pl.BlockSpec(memory_space=pl.ANY)],
        out_specs=pl.BlockSpec((1, MOBA_HEADS, HEAD_DIM), lambda n, pt, ix: (n, 0, 0)),
        scratch_shapes=[pltpu.VMEM((MOBA_HEADS, MOBA_TOPK * ppb, PAGE, HEAD_DIM), F32),
                        pltpu.VMEM((MOBA_HEADS, MOBA_TOPK * ppb, PAGE, HEAD_DIM), F32),
                        pltpu.SemaphoreType.DMA((2,))])
    return pl.pallas_call(
        kern, grid_spec=grid_spec, out_shape=jax.ShapeDtypeStruct((n_seq, MOBA_HEADS, HEAD_DIM), F32),
        compiler_params=_params(("arbitrary",)), name="moba_decode",
    )(page_table.reshape(-1), idx.reshape(-1), q_s, k_new, v_new, cache_k, cache_v)


def _dispatch_kernel(dest_ref, h_ref, out_ref, sem_ref):
    i = pl.program_id(0)

    def copy(r, k):
        d = dest_ref[2 * (i * ROW_TILE + r) + k]
        return pltpu.make_async_copy(h_ref.at[pl.ds(r, 1), :], out_ref.at[pl.ds(d, 1), :], sem_ref.at[0])

    def start(r, _):
        copy(r, 0).start()
        copy(r, 1).start()
        return 0

    def wait(r, _):
        copy(r, 0).wait()
        copy(r, 1).wait()
        return 0

    lax.fori_loop(0, ROW_TILE, start, 0)
    lax.fori_loop(0, ROW_TILE, wait, 0)


def moe_dispatch(h, dest):
    t, d = h.shape
    grid_spec = pltpu.PrefetchScalarGridSpec(
        num_scalar_prefetch=1, grid=(t // ROW_TILE,),
        in_specs=[pl.BlockSpec((ROW_TILE, d), lambda i, dr: (i, 0))],
        out_specs=pl.BlockSpec(memory_space=pl.ANY),
        scratch_shapes=[pltpu.SemaphoreType.DMA((1,))])
    return pl.pallas_call(
        _dispatch_kernel, grid_spec=grid_spec, out_shape=jax.ShapeDtypeStruct((2 * t, d), h.dtype),
        compiler_params=_params(("arbitrary",)), name="moe_dispatch",
    )(dest.reshape(-1), h)


def _experts_kernel(tile_ref, exp_ref, lo_ref, hi_ref, first_ref, newexp_ref,
                    x_ref, wg_ref, wu_ref, wd_ref, o_ref, wg_bf, wu_bf, wd_bf, *, tm):
    i = pl.program_id(0)

    @pl.when(newexp_ref[i] == 1)
    def _():
        wg_bf[...] = wg_ref[...].astype(BF16)
        wu_bf[...] = wu_ref[...].astype(BF16)
        wd_bf[...] = wd_ref[...].astype(BF16)

    lo = lo_ref[i]
    hi = hi_ref[i]

    @pl.when(hi > lo)
    def _():
        x = x_ref[...].astype(BF16)
        a = jnp.dot(x, wg_bf[...], preferred_element_type=F32)
        a = a * jax.nn.sigmoid(a) * jnp.dot(x, wu_bf[...], preferred_element_type=F32)
        y = jnp.dot(a.astype(BF16), wd_bf[...], preferred_element_type=F32)
        row = tile_ref[i] * tm + lax.broadcasted_iota(I32, (tm, 1), 0)
        y = jnp.where((row >= lo) & (row < hi), y, 0.0)

        @pl.when(first_ref[i] == 1)
        def _():
            o_ref[...] = y

        @pl.when(first_ref[i] == 0)
        def _():
            o_ref[...] = o_ref[...] + y


def moe_experts(xs, items, w_gate, w_up, w_down, layer, tm=256):
    p, d = xs.shape
    n_items = items[0].shape[0]
    ff = w_gate.shape[-1]
    kern = functools.partial(_experts_kernel, tm=tm)
    grid_spec = pltpu.PrefetchScalarGridSpec(
        num_scalar_prefetch=6, grid=(n_items,),
        in_specs=[pl.BlockSpec((tm, d), lambda i, tl, ex, *_: (tl[i], 0)),
                  pl.BlockSpec((None, None, d, ff), lambda i, tl, ex, *_: (layer, ex[i], 0, 0)),
                  pl.BlockSpec((None, None, d, ff), lambda i, tl, ex, *_: (layer, ex[i], 0, 0)),
                  pl.BlockSpec((None, None, ff, d), lambda i, tl, ex, *_: (layer, ex[i], 0, 0))],
        out_specs=pl.BlockSpec((tm, d), lambda i, tl, ex, *_: (tl[i], 0)),
        scratch_shapes=[pltpu.VMEM((d, ff), BF16), pltpu.VMEM((d, ff), BF16), pltpu.VMEM((ff, d), BF16)])
    n_e = w_gate.shape[1] * w_gate.shape[2]
    wg = w_gate.reshape(w_gate.shape[0], n_e, d, ff)
    wu = w_up.reshape(w_up.shape[0], n_e, d, ff)
    wd = w_down.reshape(w_down.shape[0], n_e, ff, d)
    return pl.pallas_call(
        kern, grid_spec=grid_spec, out_shape=jax.ShapeDtypeStruct((p, d), F32),
        compiler_params=_params(("arbitrary",)), name="moe_experts",
    )(*items, xs, wg, wu, wd)


def _combine_kernel(dest_ref, slab_ref, ys_ref, o_ref, buf_ref, sem_ref):
    i = pl.program_id(0)

    def copy(r, k):
        d = dest_ref[2 * (i * ROW_TILE + r) + k]
        return pltpu.make_async_copy(ys_ref.at[pl.ds(d, 1), :], buf_ref.at[k, pl.ds(r, 1), :], sem_ref.at[0])

    def start(r, _):
        copy(r, 0).start()
        copy(r, 1).start()
        return 0

    def wait(r, _):
        copy(r, 0).wait()
        copy(r, 1).wait()
        return 0

    lax.fori_loop(0, ROW_TILE, start, 0)
    lax.fori_loop(0, ROW_TILE, wait, 0)
    slab = slab_ref[...]
    o_ref[...] = slab[:, 2:3] * buf_ref[0] + slab[:, 3:4] * buf_ref[1]


def moe_combine(ys, dest, slab):
    t = slab.shape[0]
    d = ys.shape[1]
    grid_spec = pltpu.PrefetchScalarGridSpec(
        num_scalar_prefetch=1, grid=(t // ROW_TILE,),
        in_specs=[pl.BlockSpec((ROW_TILE, LANES), lambda i, dr: (i, 0)), pl.BlockSpec(memory_space=pl.ANY)],
        out_specs=pl.BlockSpec((ROW_TILE, d), lambda i, dr: (i, 0)),
        scratch_shapes=[pltpu.VMEM((2, ROW_TILE, d), F32), pltpu.SemaphoreType.DMA((1,))])
    return pl.pallas_call(
        _combine_kernel, grid_spec=grid_spec, out_shape=jax.ShapeDtypeStruct((t, d), F32),
        compiler_params=_params(("arbitrary",)), name="moe_combine",
    )(dest.reshape(-1), slab, ys)


def _work_items(counts, n_rows, tm):
    n_tiles = n_rows // tm
    n_items = n_tiles + N_EXPERTS - 1
    ends = jnp.cumsum(counts)
    starts = ends - counts
    tile_lo = jnp.arange(n_tiles, dtype=I32) * tm
    first_e = jnp.searchsorted(ends, tile_lo, side="right").astype(I32)
    last_e = jnp.searchsorted(ends, tile_lo + tm - 1, side="right").astype(I32)
    per_tile = last_e - first_e + 1
    item_end = jnp.cumsum(per_tile)
    item_start = item_end - per_tile
    ids = jnp.arange(n_items, dtype=I32)
    valid = ids < item_end[-1]
    tile = jnp.minimum(jnp.searchsorted(item_end, ids, side="right").astype(I32), n_tiles - 1)
    expert = jnp.where(valid, first_e[tile] + ids - item_start[tile], N_EXPERTS - 1)
    expert = jnp.clip(expert, 0, N_EXPERTS - 1)
    lo = jnp.where(valid, jnp.maximum(starts[expert], tile * tm), 0)
    hi = jnp.where(valid, jnp.minimum(ends[expert], (tile + 1) * tm), 0)
    first = jnp.where(valid & (ids == item_start[tile]), 1, 0)
    prev = jnp.concatenate([jnp.full((1,), -1, I32), expert[:-1]])
    newexp = jnp.where(expert != prev, 1, 0)
    return tuple(a.astype(I32) for a in (tile, expert, lo, hi, first, newexp))


def hier_moe(h, slab, counts, w_gate, w_up, w_down, layer, tm=256):
    counts = counts[0, :N_EXPERTS].astype(I32)
    offs = jnp.cumsum(counts) - counts
    e = slab[:, 0:2].astype(I32)
    rank = slab[:, 4:6].astype(I32)
    dest = offs[e] + rank
    xs = moe_dispatch(h, dest)
    items = _work_items(counts, xs.shape[0], tm)
    ys = moe_experts(xs, items, w_gate, w_up, w_down, layer, tm)
    return moe_combine(ys, dest, slab)


def _rope_tables(pos, half):
    inv = ROPE_THETA ** (-jnp.arange(half, dtype=F32) / half)
    ang = pos.astype(F32)[:, None] * inv[None, :]
    cos, sin = jnp.cos(ang), jnp.sin(ang)
    reps = LANES // (2 * half)
    return jnp.tile(jnp.concatenate([cos, cos], axis=1), (1, reps)), jnp.tile(jnp.concatenate([-sin, sin], axis=1), (1, reps))


def kernel(x_prompt, x_sample, cache_mla, state_conv, cache_moba_k, cache_moba_v, page_table, c_prompt, c_sample, ada_w, ada_b, norm_mix_g, norm_ffn_g, final_g, ev_w_in, ev_g_q, ev_w_qb, ev_g_kv, ev_w_kvb, ev_conv_w, ev_conv_b, ev_ln_g, ev_ln_b, ev_w_out, od_w_qkv, od_w_o, moe_w_group, moe_b_group, moe_w_sec, moe_b_sec, moe_w_gate, moe_w_up, moe_w_down):
    n_batch, seq, d = x_prompt.shape
    n_seq, dec_seq, _ = x_sample.shape
    assert dec_seq == 1 and n_seq == ROW_TILE and seq % MOBA_BLOCK == 0
    n_pages = page_table.shape[1]
    assert (n_pages * PAGE) % MOBA_BLOCK == 0
    depth = ada_w.shape[0]
    n_prompt = n_batch * seq
    t = n_prompt + n_seq
    sample_block = n_prompt // ROW_TILE
    geom = (n_prompt // ROW_TILE, seq // ROW_TILE, n_batch, n_seq)
    tm = 640 if t % 640 == 0 else ROW_TILE

    x = jnp.concatenate([x_prompt.reshape(n_prompt, d), x_sample.reshape(n_seq, d)], axis=0)
    pad = (-(n_seq + n_batch)) % 8
    c_all = jnp.concatenate([c_sample, c_prompt, jnp.zeros((pad, d), F32)], axis=0)
    mod = ada_modulation(c_all, ada_w, ada_b)
    pos = jnp.concatenate([jnp.tile(jnp.arange(seq), n_batch), jnp.full((n_seq,), n_pages * PAGE)])
    cos128, sin128 = _rope_tables(pos, HEAD_DIM // 2)
    cos64, sin64 = _rope_tables(pos, MLA_ROPE // 2)

    outs = {k: [] for k in ("mla_p", "mla_s", "conv_p", "conv_s", "k_p", "v_p", "k_s", "v_s")}
    moe_out = None
    for layer in range(depth):
        i = layer // 2
        if layer == 0:
            (h,) = resid_prenorm(x, norm_mix_g[layer], geom, mod=mod, shift=(layer, 0), scale=(layer, 1), h_dtype=BF16)
        else:
            x, h = resid_prenorm(x, norm_mix_g[layer], geom, o=moe_out, mod=mod, gate=(layer - 1, 5),
                                 shift=(layer, 0), scale=(layer, 1), h_dtype=BF16)
        if layer % 2 == 0:
            w_in = ev_w_in[i]
            w_main = jnp.concatenate([w_in[:, :Q_LORA + KV_LORA], w_in[:, Q_LORA + KV_LORA + MLA_ROPE:]], axis=1)
            w_kpe = jnp.pad(w_in[:, Q_LORA + KV_LORA:Q_LORA + KV_LORA + MLA_ROPE], ((0, 0), (0, LANES - MLA_ROPE)))
            proj = matmul(h, w_main, tm=tm, tn=512)
            kpe_raw = matmul(h, w_kpe, tm=tm, tn=LANES)
            cqn, ckvn, rows, kpe, u = even_post(proj, kpe_raw, ev_g_q[i], ev_g_kv[i], cos64, sin64, tm=ROW_TILE)
            w_qb = ev_w_qb[i].reshape(Q_LORA, MLA_HEADS, MLA_QK)
            w_q = jnp.concatenate(
                [w_qb[:, :, :MLA_NOPE].reshape(Q_LORA, MLA_HEADS * MLA_NOPE),
                 jnp.pad(w_qb[:, :, MLA_NOPE:], ((0, 0), (0, 0), (0, LANES - MLA_ROPE))).reshape(Q_LORA, MLA_HEADS * LANES)],
                axis=1)
            nope_w = MLA_HEADS * MLA_NOPE
            q = matmul(cqn, w_q, tm=tm, tn=512, out_dtype=BF16, rope=(MLA_ROPE // 2, nope_w, 2 * nope_w, cos64, sin64))
            kv = matmul(ckvn, ev_w_kvb, w_index=i, tm=512, tn=512, out_dtype=BF16, row_blocks=n_prompt // 512)
            attn_p = mla_flash(q, kv, kpe, n_batch, seq)
            q_lat = absorb_q(q, ev_w_kvb, i, sample_block).transpose(1, 0, 2)
            q_pe = q[n_prompt:, nope_w:].reshape(n_seq, MLA_HEADS, LANES).astype(F32)
            new_rows = rows[n_prompt:].reshape(n_seq, 1, KV_LORA + MLA_ROPE)
            o_lat = mla_decode(q_lat, q_pe, new_rows, cache_mla, page_table, i)
            attn_s = absorb_v(o_lat.transpose(1, 0, 2), ev_w_kvb, i)
            conv_p = conv_prompt(u, ev_conv_w[i], ev_conv_b[i], ev_ln_g[i], ev_ln_b[i], n_prompt, seq)
            u_s = u[n_prompt:]
            win = jnp.concatenate([state_conv[i], u_s[:, None, :]], axis=1)
            conv_s = conv_sample(win, ev_conv_w[i], ev_conv_b[i], ev_ln_g[i], ev_ln_b[i])
            mix = jnp.concatenate([jnp.concatenate([attn_p, conv_p], axis=1),
                                   jnp.concatenate([attn_s, conv_s], axis=1)], axis=0)
            o = matmul(mix, ev_w_out, w_index=i, tm=tm, tn=512)
            outs["mla_p"].append(rows[:n_prompt].reshape(n_batch, seq, -1))
            outs["mla_s"].append(rows[n_prompt:].reshape(n_seq, 1, -1))
            u_p = u[:n_prompt].reshape(n_batch, seq, CONV_CH)
            if seq >= CONV_WIDTH - 1:
                outs["conv_p"].append(u_p[:, seq - (CONV_WIDTH - 1):])
            else:
                outs["conv_p"].append(jnp.pad(u_p, ((0, 0), (CONV_WIDTH - 1 - seq, 0), (0, 0))))
            outs["conv_s"].append(win[:, 1:])
        else:
            nq = MOBA_HEADS * HEAD_DIM
            nk = MOBA_KV_HEADS * HEAD_DIM
            qkv = matmul(h, od_w_qkv, w_index=i, tm=tm, tn=512, rope=(HEAD_DIM // 2, 0, nq + nk, cos128, sin128))
            attn_p = moba_prompt(qkv, n_batch, seq)
            q_s = qkv[n_prompt:, :nq].reshape(n_seq, MOBA_HEADS, HEAD_DIM)
            k_s = qkv[n_prompt:, nq:nq + nk].reshape(n_seq, MOBA_KV_HEADS, HEAD_DIM)
            v_s = qkv[n_prompt:, nq + nk:].reshape(n_seq, MOBA_KV_HEADS, HEAD_DIM)
            idx = moba_gate(q_s, cache_moba_k, page_table, i)[:, :, :MOBA_TOPK]
            attn_s = moba_decode(q_s, k_s, v_s, idx, cache_moba_k, cache_moba_v, page_table, i)
            mix = jnp.concatenate([attn_p, attn_s.reshape(n_seq, nq).astype(BF16)], axis=0)
            o = matmul(mix, od_w_o, w_index=i, tm=tm, tn=512)
            outs["k_p"].append(qkv[:n_prompt, nq:nq + nk].reshape(n_batch, seq, MOBA_KV_HEADS, HEAD_DIM))
            outs["v_p"].append(qkv[:n_prompt, nq + nk:].reshape(n_batch, seq, MOBA_KV_HEADS, HEAD_DIM))
            outs["k_s"].append(k_s.reshape(n_seq, 1, MOBA_KV_HEADS, HEAD_DIM))
            outs["v_s"].append(v_s.reshape(n_seq, 1, MOBA_KV_HEADS, HEAD_DIM))
        w_r = jnp.pad(jnp.concatenate([moe_w_group[layer], moe_w_sec[layer]], axis=1),
                      ((0, 0), (0, LANES - N_GROUPS - N_EXPERTS)))
        b_r = jnp.pad(jnp.concatenate([moe_b_group[layer], moe_b_sec[layer]]), (0, LANES - N_GROUPS - N_EXPERTS))
        x, h2, slab, counts = resid_prenorm(x, norm_ffn_g[layer], geom, o=o, mod=mod, gate=(layer, 2),
                                            shift=(layer, 3), scale=(layer, 4), router=(w_r, b_r.reshape(1, LANES)))
        moe_out = hier_moe(h2, slab, counts, moe_w_gate, moe_w_up, moe_w_down, layer)
    _, y = resid_prenorm(x, final_g, geom, o=moe_out, mod=mod, gate=(depth - 1, 5))
    y_prompt = y[:n_prompt].reshape(n_batch, seq, d)
    y_sample = y[n_prompt:].reshape(n_seq, 1, d)
    return (y_prompt, y_sample, jnp.stack(outs["mla_p"]), jnp.stack(outs["mla_s"]), jnp.stack(outs["conv_p"]),
            jnp.stack(outs["conv_s"]), jnp.stack(outs["k_p"]), jnp.stack(outs["v_p"]), jnp.stack(outs["k_s"]),
            jnp.stack(outs["v_s"]))
```
